```python
import math
import jax, jax.numpy as jnp
from jax import lax
import numpy as np

D_MODEL = 1024
BATCH = 8
SEQ = 2048
DEPTH = 1

DIFF_HEADS = 4
DIFF_HEAD_DIM = 64
DIFF_V_DIM = 2 * DIFF_HEAD_DIM
DIFF_QK_W = DIFF_HEADS * 2 * DIFF_HEAD_DIM
DIFF_V_W = DIFF_HEADS * DIFF_V_DIM
RET_HEADS = 4
RET_QK_DIM = 128
RET_V_DIM = 128
RET_QK_W = RET_HEADS * RET_QK_DIM
RET_V_W = RET_HEADS * RET_V_DIM
RET_CHUNK = 128
Q_BLOCK = 128
ROPE_THETA = 10000.0
NORM_EPS = 1e-6
D_MIX = DIFF_V_W + RET_V_W
IN_WIDTHS = [DIFF_QK_W, DIFF_QK_W, DIFF_V_W, DIFF_V_W, RET_QK_W, RET_QK_W, RET_V_W, RET_V_W]
D_IN_PROJ = sum(IN_WIDTHS)
IN_SPLITS = [int(v) for v in np.cumsum(IN_WIDTHS)[:-1]]

kernel_name = "hybrid_diffattn_retention_block"


def rmsnorm(x, w=None, eps=NORM_EPS):
    xf = x.astype(jnp.float32)
    y = xf * lax.rsqrt(jnp.mean(xf * xf, axis=-1, keepdims=True) + eps)
    if w is not None:
        y = y * w.astype(jnp.float32)
    return y.astype(x.dtype)


def rotary(x, positions, theta=ROPE_THETA):
    d = x.shape[-1]
    half = d // 2
    inv_freq = theta ** (-jnp.arange(half, dtype=jnp.float32) / half)
    ang = positions.astype(jnp.float32)[..., None] * inv_freq
    ang = ang.reshape(ang.shape[:2] + (1,) * (x.ndim - 3) + (half,))
    cos, sin = jnp.cos(ang), jnp.sin(ang)
    xf = x.astype(jnp.float32)
    x1, x2 = xf[..., :half], xf[..., half:]
    out = jnp.concatenate([x1 * cos - x2 * sin, x2 * cos + x1 * sin], axis=-1)
    return out.astype(x.dtype)


def diff_attention(q, k, v, positions, lam):
    B, S, _ = q.shape
    nb = S // Q_BLOCK
    q = rotary(q.reshape(B, S, DIFF_HEADS, 2, DIFF_HEAD_DIM), positions)
    k = rotary(k.reshape(B, S, DIFF_HEADS, 2, DIFF_HEAD_DIM), positions)
    scale = DIFF_HEAD_DIM ** -0.5
    qf = (q.astype(jnp.float32) * scale).transpose(0, 2, 3, 1, 4)
    q_blocks = jnp.moveaxis(qf.reshape(B, DIFF_HEADS, 2, nb, Q_BLOCK, DIFF_HEAD_DIM), 3, 0)
    kf = k.astype(jnp.float32).transpose(0, 2, 3, 1, 4)
    vf = v.astype(jnp.float32).reshape(B, S, DIFF_HEADS, DIFF_V_DIM).transpose(0, 2, 1, 3)
    key_pos = jnp.arange(S)

    def one_block(args):
        qb, blk = args
        s = jnp.einsum('bhcqd,bhckd->bhcqk', qb, kf)
        q_pos = blk * Q_BLOCK + jnp.arange(Q_BLOCK)
        causal = key_pos[None, :] <= q_pos[:, None]
        s = jnp.where(causal, s, jnp.float32(-1e30))
        p = jax.nn.softmax(s, axis=-1)
        a = p[:, :, 0] - lam * p[:, :, 1]
        return jnp.einsum('bhqk,bhkv->bhqv', a, vf)

    out = lax.map(one_block, (q_blocks, jnp.arange(nb)))
    out = out.transpose(1, 0, 3, 2, 4).reshape(B, S, DIFF_HEADS, DIFF_V_DIM)
    return out.astype(v.dtype)


def retention_chunkwise(q, k, v, positions):
    B, S, _ = q.shape
    n = S // RET_CHUNK
    C = RET_CHUNK
    q = rotary(q.reshape(B, S, RET_HEADS, RET_QK_DIM), positions)
    k = rotary(k.reshape(B, S, RET_HEADS, RET_QK_DIM), positions)

    def chunks(t, d):
        return t.astype(jnp.float32).reshape(B, n, C, RET_HEADS, d).transpose(0, 3, 1, 2, 4)

    qc = chunks(q, RET_QK_DIM)
    kc = chunks(k, RET_QK_DIM) * (RET_QK_DIM ** -0.5)
    vc = chunks(v.reshape(B, S, RET_HEADS, RET_V_DIM), RET_V_DIM)

    log_g = jnp.log(1.0 - 2.0 ** (-5.0 - jnp.arange(RET_HEADS, dtype=jnp.float32)))
    idx = jnp.arange(C, dtype=jnp.float32)
    diff = idx[:, None] - idx[None, :]
    decay_in = jnp.where(diff >= 0, jnp.exp(jnp.maximum(diff, 0.0)[None] * log_g[:, None, None]), 0.0)
    zeta = jnp.exp((C - 1 - idx)[None] * log_g[:, None])
    xi = jnp.exp((idx + 1)[None] * log_g[:, None])
    g_chunk = jnp.exp(C * log_g)

    s = jnp.einsum('bhncd,bhnmd->bhncm', qc, kc) * decay_in[None, :, None]
    inner = jnp.einsum('bhncm,bhnme->bhnce', s, vc)
    kv = jnp.einsum('bhncd,bhnce->bhnde', kc * zeta[None, :, None, :, None], vc)

    def step(R, kv_n):
        return g_chunk[None, :, None, None] * R + kv_n, R

    R0 = jnp.zeros((B, RET_HEADS, RET_QK_DIM, RET_V_DIM), jnp.float32)
    _, R_prev = lax.scan(step, R0, jnp.moveaxis(kv, 2, 0))
    R_prev = jnp.moveaxis(R_prev, 0, 2)
    cross = jnp.einsum('bhncd,bhnde->bhnce', qc * xi[None, :, None, :, None], R_prev)
    out = (inner + cross).transpose(0, 2, 3, 1, 4).reshape(B, S, RET_HEADS, RET_V_DIM)
    return out.astype(v.dtype)


def setup_inputs(seed: int = 0) -> dict:
    key = jax.random.key(seed)
    ks = jax.random.split(key, 12)
    x = jax.random.normal(ks[0], (BATCH, SEQ, D_MODEL), jnp.float32)
    offsets = jax.random.randint(ks[1], (BATCH, 1), 0, 4096, dtype=jnp.int32)
    positions = (offsets + jnp.arange(SEQ, dtype=jnp.int32)[None, :]).astype(jnp.int32)
    norm_pre_w = 1.0 + 0.01 * jax.random.normal(ks[2], (DEPTH, D_MODEL), jnp.float32)
    w_in = jax.random.normal(ks[3], (DEPTH, D_MODEL, D_IN_PROJ), jnp.float32) * D_MODEL ** -0.5
    lambda_q1 = 0.1 * jax.random.normal(ks[4], (DEPTH, DIFF_HEAD_DIM), jnp.float32)
    lambda_k1 = 0.1 * jax.random.normal(ks[5], (DEPTH, DIFF_HEAD_DIM), jnp.float32)
    lambda_q2 = 0.1 * jax.random.normal(ks[6], (DEPTH, DIFF_HEAD_DIM), jnp.float32)
    lambda_k2 = 0.1 * jax.random.normal(ks[7], (DEPTH, DIFF_HEAD_DIM), jnp.float32)
    diff_subln_w = 1.0 + 0.01 * jax.random.normal(ks[8], (DEPTH, DIFF_V_DIM), jnp.float32)
    w_out = jax.random.normal(ks[9], (DEPTH, D_MIX, D_MODEL), jnp.float32) * D_MIX ** -0.5
    norm_post_w = 1.0 + 0.01 * jax.random.normal(ks[10], (DEPTH, D_MODEL), jnp.float32)
    return {"x": x, "positions": positions, "norm_pre_w": norm_pre_w, "w_in": w_in,
            "lambda_q1": lambda_q1, "lambda_k1": lambda_k1, "lambda_q2": lambda_q2, "lambda_k2": lambda_k2,
            "diff_subln_w": diff_subln_w, "w_out": w_out, "norm_post_w": norm_post_w}


def reference(x, positions, norm_pre_w, w_in, lambda_q1, lambda_k1, lambda_q2, lambda_k2,
              diff_subln_w, w_out, norm_post_w):
    B, S, _ = x.shape
    h = x
    for layer in range(DEPTH):
        lambda_init = 0.8 - 0.6 * math.exp(-0.3 * layer)
        u = rmsnorm(h, norm_pre_w[layer])
        proj = jnp.einsum('bsd,de->bse', u, w_in[layer])
        dq, dk, dv, dg, rq, rk, rv, rg = jnp.split(proj, IN_SPLITS, axis=-1)

        lam = (jnp.exp(jnp.sum(lambda_q1[layer].astype(jnp.float32) * lambda_k1[layer].astype(jnp.float32)))
               - jnp.exp(jnp.sum(lambda_q2[layer].astype(jnp.float32) * lambda_k2[layer].astype(jnp.float32)))
               + lambda_init)
        y_a = diff_attention(dq, dk, dv, positions, lam)
        y_a = rmsnorm(y_a, diff_subln_w[layer]) * (1.0 - lambda_init)
        y_a = y_a.reshape(B, S, DIFF_V_W) * jax.nn.silu(dg)

        y_b = retention_chunkwise(rq, rk, rv, positions)
        y_b = rmsnorm(y_b).reshape(B, S, RET_V_W) * jax.nn.silu(rg)

        y = jnp.einsum('bsm,md->bsd', jnp.concatenate([y_a, y_b], axis=-1), w_out[layer])
        h = h + rmsnorm(y, norm_post_w[layer])
    return h
```

```python
import functools
import math

import jax
import jax.numpy as jnp
from jax import lax
from jax.experimental import pallas as pl
from jax.experimental.pallas import tpu as pltpu

LANES = 128

D_MODEL = 1024
DIFF_HEADS = 4
DIFF_HEAD_DIM = 64
RET_HEADS = 4
RET_QK_DIM = 128
SECTION = 512
N_SECTIONS = 8
D_IN_PROJ = SECTION * N_SECTIONS
ROPE_THETA = 10000.0
NORM_EPS = 1e-6
LAMBDA_INIT = 0.8 - 0.6 * math.exp(-0.3 * 0)
NEG_BIG = -1e30

TM = 512
TQ = 256
TK = 256
RET_CHUNK = 128

_Q_D, _K_D, _G_D, _Q_R, _K_R, _V_R, _G_R = (i * (SECTION // LANES) for i in range(7))

VMEM_LIMIT = 56 * 1024 * 1024


def _silu(g):
    return g * (1.0 / (1.0 + jnp.exp(-g)))


def _rms(y):
    return y * lax.rsqrt(jnp.mean(y * y, axis=-1, keepdims=True) + NORM_EPS)


def _inproj_kernel(x_ref, pos_ref, gpre_ref, w_ref, proj_ref, vt_ref):
    x = x_ref[...]
    u = (_rms(x) * gpre_ref[...]).astype(jnp.bfloat16)

    lane = lax.broadcasted_iota(jnp.int32, (1, LANES), 1)
    lo64 = lane < 64
    expo = jnp.where(lo64, lane.astype(jnp.float32) * (1.0 / 64.0),
                     (lane & 31).astype(jnp.float32) * (1.0 / 32.0))
    inv_freq = jnp.exp(expo * (-math.log(ROPE_THETA)))
    ang = pos_ref[...].astype(jnp.float32) * inv_freq
    cos_a = jnp.cos(ang)
    sin_a = jnp.sin(ang)
    cos_sw = pltpu.roll(cos_a, 64, 1)
    sin_sw = pltpu.roll(sin_a, 64, 1)
    cos_r = jnp.where(lo64, cos_a, cos_sw)
    sin_r = jnp.where(lo64, -sin_a, sin_sw)
    lo32 = (lane & 32) == 0
    cos_d = jnp.where(lo64, cos_sw, cos_a)
    sin_dd = jnp.where(lo64, sin_sw, sin_a)
    sin_d = jnp.where(lo32, -sin_dd, sin_dd)

    def section(sec):
        return jnp.dot(u, w_ref[:, sec * SECTION:(sec + 1) * SECTION],
                       preferred_element_type=jnp.float32)

    def rot_diff(blk):
        sw = jnp.where(lo32, pltpu.roll(blk, LANES - 32, 1), pltpu.roll(blk, 32, 1))
        return blk * cos_d + sw * sin_d

    def rot_ret(blk):
        return blk * cos_r + pltpu.roll(blk, 64, 1) * sin_r

    def store(t, dst_sec, fn):
        for hb in range(SECTION // LANES):
            blk = fn(t[:, hb * LANES:(hb + 1) * LANES])
            c0 = dst_sec * SECTION + hb * LANES
            proj_ref[:, c0:c0 + LANES] = blk.astype(jnp.bfloat16)

    ident = lambda b: b
    store(section(0), 0, lambda b: rot_diff(b) * (DIFF_HEAD_DIM ** -0.5))
    store(section(1), 1, rot_diff)
    vt = section(2).T
    for c in range(TM // TK):
        vt_ref[c] = vt[:, c * TK:(c + 1) * TK].astype(jnp.bfloat16)
    store(section(3), 2, ident)
    store(section(4), 3, rot_ret)
    store(section(5), 4, lambda b: rot_ret(b) * (RET_QK_DIM ** -0.5))
    store(section(6), 5, ident)
    store(section(7), 6, ident)


def _inproj(x2, pos2, gpre, w_bf16):
    T = x2.shape[0]
    return pl.pallas_call(
        _inproj_kernel,
        grid=(T // TM,),
        in_specs=[
            pl.BlockSpec((TM, D_MODEL), lambda i: (i, 0)),
            pl.BlockSpec((TM, 1), lambda i: (i, 0)),
            pl.BlockSpec((1, D_MODEL), lambda i: (0, 0)),
            pl.BlockSpec((D_MODEL, D_IN_PROJ), lambda i: (0, 0)),
        ],
        out_specs=[
            pl.BlockSpec((TM, 7 * SECTION), lambda i: (i, 0)),
            pl.BlockSpec((TM // TK, SECTION, TK), lambda i: (i, 0, 0)),
        ],
        out_shape=[
            jax.ShapeDtypeStruct((T, 7 * SECTION), jnp.bfloat16),
            jax.ShapeDtypeStruct((T // TK, SECTION, TK), jnp.bfloat16),
        ],
        compiler_params=pltpu.CompilerParams(
            dimension_semantics=("arbitrary",), vmem_limit_bytes=VMEM_LIMIT),
        name="inproj",
    )(x2, pos2, gpre, w_bf16)


def _diff_attn_kernel(lamv_ref, w_ref, q_ref, k_ref, vt_ref, g_ref, o_ref, *, seq):
    lv = lamv_ref[...]
    lam = (jnp.exp(jnp.sum(lv[0:1] * lv[1:2], axis=-1, keepdims=True))
           - jnp.exp(jnp.sum(lv[2:3] * lv[3:4], axis=-1, keepdims=True)) + LAMBDA_INIT)

    lane_q = lax.broadcasted_iota(jnp.int32, (TQ, LANES), 1)
    causal = (lax.broadcasted_iota(jnp.int32, (TK, TQ), 0)
              <= lax.broadcasted_iota(jnp.int32, (TK, TQ), 1))
    nt = (((1,), (1,)), ((), ()))

    def q_block(qi, carry):
        qs = pl.multiple_of(qi * TQ, TQ)
        q = q_ref[pl.ds(qs, TQ), :]
        zero = jnp.zeros_like(q)
        qc = (jnp.where(lane_q < DIFF_HEAD_DIM, q, zero), jnp.where(lane_q >= DIFF_HEAD_DIM, q, zero))

        def kv_step(j, state, masked):
            kj = k_ref[pl.ds(pl.multiple_of(j * TK, TK), TK), :]
            vtj = vt_ref[j]
            new = []
            for c in range(2):
                m, l, acc = state[c]
                s = lax.dot_general(kj, qc[c], nt, preferred_element_type=jnp.float32)
                if masked:
                    s = jnp.where(causal, s, NEG_BIG)
                m_new = jnp.maximum(m, jnp.max(s, axis=0, keepdims=True))
                alpha = jnp.exp(m - m_new)
                p = jnp.exp(s - m_new)
                l = alpha * l + jnp.sum(p, axis=0, keepdims=True)
                acc = alpha * acc + jnp.dot(vtj, p.astype(jnp.bfloat16),
                                            preferred_element_type=jnp.float32)
                new.append((m_new, l, acc))
            return tuple(new)

        init = tuple((jnp.full((1, TQ), NEG_BIG, jnp.float32), jnp.zeros((1, TQ), jnp.float32),
                      jnp.zeros((LANES, TQ), jnp.float32)) for _ in range(2))
        state = lax.fori_loop(0, qi, lambda j, st: kv_step(j, st, False), init)
        (_, l1, a1), (_, l2, a2) = kv_step(qi, state, True)

        y = (a1 * (1.0 / l1) - lam * (a2 * (1.0 / l2))).T
        y = _rms(y) * w_ref[...] * (1.0 - LAMBDA_INIT)
        g = g_ref[pl.ds(qs, TQ), :].astype(jnp.float32)
        o_ref[pl.ds(qs, TQ), :] = (y * _silu(g)).astype(jnp.bfloat16)
        return carry

    lax.fori_loop(0, seq // TQ, q_block, 0)


def _diff_attn(lamv, subln_w, proj, vt, batch, seq):
    nkv = seq // TK
    return pl.pallas_call(
        functools.partial(_diff_attn_kernel, seq=seq),
        grid=(batch, DIFF_HEADS),
        in_specs=[
            pl.BlockSpec((4, DIFF_HEAD_DIM), lambda b, h: (0, 0)),
            pl.BlockSpec((1, LANES), lambda b, h: (0, 0)),
            pl.BlockSpec((seq, LANES), lambda b, h: (b, _Q_D + h)),
            pl.BlockSpec((seq, LANES), lambda b, h: (b, _K_D + h)),
            pl.BlockSpec((nkv, LANES, TK), lambda b, h: (b, h, 0)),
            pl.BlockSpec((seq, LANES), lambda b, h: (b, _G_D + h)),
        ],
        out_specs=pl.BlockSpec((seq, LANES), lambda b, h: (b, h)),
        out_shape=jax.ShapeDtypeStruct((batch * seq, DIFF_HEADS * LANES), jnp.bfloat16),
        compiler_params=pltpu.CompilerParams(
            dimension_semantics=("arbitrary", "arbitrary"), vmem_limit_bytes=VMEM_LIMIT),
        name="diff_attn",
    )(lamv, subln_w, proj, proj, vt, proj)


def _retention_kernel(q_ref, k_ref, v_ref, g_ref, o_ref, *, seq):
    C = RET_CHUNK
    h = pl.program_id(1)
    pow2 = jnp.full((1, 1), lax.shift_left(jnp.int32(1), 5 + h), jnp.int32).astype(jnp.float32)
    log_g = jnp.log(1.0 - 1.0 / pow2)
    ri = lax.broadcasted_iota(jnp.int32, (C, C), 0)
    ci = lax.broadcasted_iota(jnp.int32, (C, C), 1)
    d = (ri - ci).astype(jnp.float32)
    decay_in = jnp.where(d >= 0, jnp.exp(jnp.maximum(d, 0.0) * log_g), 0.0)
    rf = lax.broadcasted_iota(jnp.int32, (C, LANES), 0).astype(jnp.float32)
    zeta = jnp.exp((C - 1 - rf) * log_g)
    xi = jnp.exp((rf + 1.0) * log_g)
    g_chunk = jnp.exp(C * log_g)
    nt = (((1,), (1,)), ((), ()))

    def chunk(n, R):
        rs = pl.multiple_of(n * C, C)
        qc = q_ref[pl.ds(rs, C), :]
        kc = k_ref[pl.ds(rs, C), :]
        vc = v_ref[pl.ds(rs, C), :]
        s = lax.dot_general(qc, kc, nt, preferred_element_type=jnp.float32) * decay_in
        inner = jnp.dot(s.astype(jnp.bfloat16), vc, preferred_element_type=jnp.float32)
        cross = jnp.dot(qc, R.astype(jnp.bfloat16), preferred_element_type=jnp.float32) * xi
        kzt = (kc.astype(jnp.float32) * zeta).T.astype(jnp.bfloat16)
        kv = jnp.dot(kzt, vc, preferred_element_type=jnp.float32)
        y = _rms(inner + cross)
        g = g_ref[pl.ds(rs, C), :].astype(jnp.float32)
        o_ref[pl.ds(rs, C), :] = (y * _silu(g)).astype(jnp.bfloat16)
        return g_chunk * R + kv

    lax.fori_loop(0, seq // C, chunk, jnp.zeros((RET_QK_DIM, LANES), jnp.float32))


def _retention(proj, batch, seq):
    spec = lambda off: pl.BlockSpec((seq, LANES), lambda b, h: (b, off + h))
    return pl.pallas_call(
        functools.partial(_retention_kernel, seq=seq),
        grid=(batch, RET_HEADS),
        in_specs=[spec(_Q_R), spec(_K_R), spec(_V_R), spec(_G_R)],
        out_specs=pl.BlockSpec((seq, LANES), lambda b, h: (b, h)),
        out_shape=jax.ShapeDtypeStruct((batch * seq, RET_HEADS * LANES), jnp.bfloat16),
        compiler_params=pltpu.CompilerParams(
            dimension_semantics=("arbitrary", "arbitrary"), vmem_limit_bytes=VMEM_LIMIT),
        name="retention",
    )(proj, proj, proj, proj)


def _outproj_kernel(ya_ref, yb_ref, w_ref, gpost_ref, x_ref, o_ref):
    half = w_ref.shape[0] // 2
    y = (jnp.dot(ya_ref[...], w_ref[:half, :], preferred_element_type=jnp.float32)
         + jnp.dot(yb_ref[...], w_ref[half:, :], preferred_element_type=jnp.float32))
    o_ref[...] = x_ref[...] + _rms(y) * gpost_ref[...]


def _outproj(ya, yb, w_bf16, gpost, x2):
    T = x2.shape[0]
    d_mix = w_bf16.shape[0]
    return pl.pallas_call(
        _outproj_kernel,
        grid=(T // TM,),
        in_specs=[
            pl.BlockSpec((TM, d_mix // 2), lambda i: (i, 0)),
            pl.BlockSpec((TM, d_mix // 2), lambda i: (i, 0)),
            pl.BlockSpec((d_mix, D_MODEL), lambda i: (0, 0)),
            pl.BlockSpec((1, D_MODEL), lambda i: (0, 0)),
            pl.BlockSpec((TM, D_MODEL), lambda i: (i, 0)),
        ],
        out_specs=pl.BlockSpec((TM, D_MODEL), lambda i: (i, 0)),
        out_shape=jax.ShapeDtypeStruct((T, D_MODEL), jnp.float32),
        compiler_params=pltpu.CompilerParams(
            dimension_semantics=("arbitrary",), vmem_limit_bytes=VMEM_LIMIT),
        name="outproj",
    )(ya, yb, w_bf16, gpost, x2)


def kernel(x, positions, norm_pre_w, w_in, lambda_q1, lambda_k1, lambda_q2, lambda_k2,
           diff_subln_w, w_out, norm_post_w):
    B, S, D = x.shape
    assert D == D_MODEL and w_in.shape == (1, D_MODEL, D_IN_PROJ) and S % TM == 0
    x2 = x.reshape(B * S, D)
    pos2 = positions.reshape(B * S, 1)
    proj, vt = _inproj(x2, pos2, norm_pre_w, w_in[0].astype(jnp.bfloat16))
    lamv = jnp.concatenate([lambda_q1, lambda_k1, lambda_q2, lambda_k2], axis=0)
    ya = _diff_attn(lamv, diff_subln_w, proj, vt, B, S)
    yb = _retention(proj, B, S)
    out = _outproj(ya, yb, w_out[0].astype(jnp.bfloat16), norm_post_w, x2)
    return out.reshape(B, S, D)
```

```python
import functools
import math

import jax
import jax.numpy as jnp
from jax import lax
from jax.experimental import pallas as pl
from jax.experimental.pallas import tpu as pltpu

LANES = 128

D_MODEL = 1024
DIFF_HEADS = 4
DIFF_HEAD_DIM = 64
RET_HEADS = 4
RET_QK_DIM = 128
SECTION = 512
N_SECTIONS = 8
D_IN_PROJ = SECTION * N_SECTIONS
ROPE_THETA = 10000.0
NORM_EPS = 1e-6
LAMBDA_INIT = 0.8 - 0.6 * math.exp(-0.3 * 0)
NEG_BIG = -1e30
LOG2_E = math.log2(math.e)

TM = 512
TQ = 256
RET_CHUNK = 128

_Q_D, _K_D, _G_D, _Q_R, _K_R, _V_R, _G_R = (i * (SECTION // LANES) for i in range(7))

VMEM_LIMIT = 56 * 1024 * 1024


def _silu(g):
    return g * (1.0 / (1.0 + jnp.exp(-g)))


def _rms(y):
    return y * lax.rsqrt(jnp.mean(y * y, axis=-1, keepdims=True) + NORM_EPS)


def _inproj_kernel(x_ref, pos_ref, gpre_ref, w_ref, proj_ref, vt_ref):
    x = x_ref[...]
    u = (_rms(x) * gpre_ref[...]).astype(jnp.bfloat16)

    lane = lax.broadcasted_iota(jnp.int32, (1, LANES), 1)
    lo64 = lane < 64
    expo = jnp.where(lo64, lane.astype(jnp.float32) * (1.0 / 64.0),
                     (lane & 31).astype(jnp.float32) * (1.0 / 32.0))
    inv_freq = jnp.exp(expo * (-math.log(ROPE_THETA)))
    ang = pos_ref[...].astype(jnp.float32) * inv_freq
    cos_a = jnp.cos(ang)
    sin_a = jnp.sin(ang)
    cos_sw = pltpu.roll(cos_a, 64, 1)
    sin_sw = pltpu.roll(sin_a, 64, 1)
    cos_r = jnp.where(lo64, cos_a, cos_sw)
    sin_r = jnp.where(lo64, -sin_a, sin_sw)
    lo32 = (lane & 32) == 0
    cos_d = jnp.where(lo64, cos_sw, cos_a)
    sin_dd = jnp.where(lo64, sin_sw, sin_a)
    sin_d = jnp.where(lo32, -sin_dd, sin_dd)

    def section(sec):
        return jnp.dot(u, w_ref[:, sec * SECTION:(sec + 1) * SECTION],
                       preferred_element_type=jnp.float32)

    def rot_diff(blk):
        sw = jnp.where(lo32, pltpu.roll(blk, LANES - 32, 1), pltpu.roll(blk, 32, 1))
        return blk * cos_d + sw * sin_d

    def rot_ret(blk):
        return blk * cos_r + pltpu.roll(blk, 64, 1) * sin_r

    def store(t, dst_sec, fn):
        for hb in range(SECTION // LANES):
            blk = fn(t[:, hb * LANES:(hb + 1) * LANES])
            c0 = dst_sec * SECTION + hb * LANES
            proj_ref[:, c0:c0 + LANES] = blk.astype(jnp.bfloat16)

    ident = lambda b: b
    store(section(0), 0, lambda b: rot_diff(b) * (DIFF_HEAD_DIM ** -0.5 * LOG2_E))
    store(section(1), 1, rot_diff)
    vt_ref[...] = section(2).T.astype(jnp.bfloat16)
    store(section(3), 2, ident)
    store(section(4), 3, rot_ret)
    store(section(5), 4, lambda b: rot_ret(b) * (RET_QK_DIM ** -0.5))
    store(section(6), 5, ident)
    store(section(7), 6, ident)


def _inproj(x2, pos2, gpre, w_bf16, batch, seq):
    T = x2.shape[0]
    nblk = seq // TM
    return pl.pallas_call(
        _inproj_kernel,
        grid=(T // TM,),
        in_specs=[
            pl.BlockSpec((TM, D_MODEL), lambda i: (i, 0)),
            pl.BlockSpec((TM, 1), lambda i: (i, 0)),
            pl.BlockSpec((1, D_MODEL), lambda i: (0, 0)),
            pl.BlockSpec((D_MODEL, D_IN_PROJ), lambda i: (0, 0)),
        ],
        out_specs=[
            pl.BlockSpec((TM, 7 * SECTION), lambda i: (i, 0)),
            pl.BlockSpec((None, SECTION, TM), lambda i: (i // nblk, 0, i % nblk)),
        ],
        out_shape=[
            jax.ShapeDtypeStruct((T, 7 * SECTION), jnp.bfloat16),
            jax.ShapeDtypeStruct((batch, SECTION, seq), jnp.bfloat16),
        ],
        compiler_params=pltpu.CompilerParams(
            dimension_semantics=("arbitrary",), vmem_limit_bytes=VMEM_LIMIT),
        name="inproj",
    )(x2, pos2, gpre, w_bf16)


def _diff_attn_kernel(lamv_ref, w_ref, q_ref, k_ref, vt_ref, g_ref, o_ref, *, seq):
    lv = lamv_ref[...]
    lam = (jnp.exp(jnp.sum(lv[0:1] * lv[1:2], axis=-1, keepdims=True))
           - jnp.exp(jnp.sum(lv[2:3] * lv[3:4], axis=-1, keepdims=True)) + LAMBDA_INIT)

    lane_q = lax.broadcasted_iota(jnp.int32, (TQ, LANES), 1)
    causal = (lax.broadcasted_iota(jnp.int32, (TQ, TQ), 0)
              <= lax.broadcasted_iota(jnp.int32, (TQ, TQ), 1))
    nt = (((1,), (1,)), ((), ()))

    for qi in range(seq // TQ):
        qs = qi * TQ
        q = q_ref[qs:qs + TQ, :]
        zero = jnp.zeros_like(q)
        outs = []
        for c in range(2):
            qc = jnp.where((lane_q >= DIFF_HEAD_DIM) == bool(c), q, zero)
            s_d = lax.dot_general(k_ref[qs:qs + TQ, :], qc, nt, preferred_element_type=jnp.float32)
            s_d = jnp.where(causal, s_d, NEG_BIG)
            m = jnp.max(s_d, axis=0, keepdims=True)
            if qi > 0:
                s_o = lax.dot_general(k_ref[0:qs, :], qc, nt, preferred_element_type=jnp.float32)
                m = jnp.maximum(m, jnp.max(s_o, axis=0, keepdims=True))
            p_d = jnp.exp2(s_d - m)
            l = jnp.sum(p_d, axis=0, keepdims=True)
            acc = jnp.dot(vt_ref[:, qs:qs + TQ], p_d.astype(jnp.bfloat16),
                          preferred_element_type=jnp.float32)
            if qi > 0:
                p_o = jnp.exp2(s_o - m)
                l = l + jnp.sum(p_o, axis=0, keepdims=True)
                acc = acc + jnp.dot(vt_ref[:, 0:qs], p_o.astype(jnp.bfloat16),
                                    preferred_element_type=jnp.float32)
            outs.append(acc * (1.0 / l))

        y = (outs[0] - lam * outs[1]).T
        y = _rms(y) * w_ref[...] * (1.0 - LAMBDA_INIT)
        g = g_ref[qs:qs + TQ, :].astype(jnp.float32)
        o_ref[qs:qs + TQ, :] = (y * _silu(g)).astype(jnp.bfloat16)


def _diff_attn(lamv, subln_w, proj, vt, batch, seq):
    return pl.pallas_call(
        functools.partial(_diff_attn_kernel, seq=seq),
        grid=(batch, DIFF_HEADS),
        in_specs=[
            pl.BlockSpec((4, DIFF_HEAD_DIM), lambda b, h: (0, 0)),
            pl.BlockSpec((1, LANES), lambda b, h: (0, 0)),
            pl.BlockSpec((seq, LANES), lambda b, h: (b, _Q_D + h)),
            pl.BlockSpec((seq, LANES), lambda b, h: (b, _K_D + h)),
            pl.BlockSpec((None, LANES, seq), lambda b, h: (b, h, 0)),
            pl.BlockSpec((seq, LANES), lambda b, h: (b, _G_D + h)),
        ],
        out_specs=pl.BlockSpec((seq, LANES), lambda b, h: (b, h)),
        out_shape=jax.ShapeDtypeStruct((batch * seq, DIFF_HEADS * LANES), jnp.bfloat16),
        compiler_params=pltpu.CompilerParams(
            dimension_semantics=("arbitrary", "arbitrary"), vmem_limit_bytes=VMEM_LIMIT),
        name="diff_attn",
    )(lamv, subln_w, proj, proj, vt, proj)


def _retention_kernel(q_ref, k_ref, v_ref, g_ref, o_ref, *, seq):
    C = RET_CHUNK
    h = pl.program_id(1)
    pow2 = jnp.full((1, 1), lax.shift_left(jnp.int32(1), 5 + h), jnp.int32).astype(jnp.float32)
    log_g = jnp.log(1.0 - 1.0 / pow2)
    ri = lax.broadcasted_iota(jnp.int32, (C, C), 0)
    ci = lax.broadcasted_iota(jnp.int32, (C, C), 1)
    d = (ri - ci).astype(jnp.float32)
    decay_in = jnp.where(d >= 0, jnp.exp(jnp.maximum(d, 0.0) * log_g), 0.0)
    rf = lax.broadcasted_iota(jnp.int32, (C, LANES), 0).astype(jnp.float32)
    zeta = jnp.exp((C - 1 - rf) * log_g)
    xi = jnp.exp((rf + 1.0) * log_g)
    g_chunk = jnp.exp(C * log_g)
    nt = (((1,), (1,)), ((), ()))

    R = jnp.zeros((RET_QK_DIM, LANES), jnp.float32)
    for n in range(seq // C):
        rs = n * C
        qc = q_ref[rs:rs + C, :]
        kc = k_ref[rs:rs + C, :]
        vc = v_ref[rs:rs + C, :]
        s = lax.dot_general(qc, kc, nt, preferred_element_type=jnp.float32) * decay_in
        o = jnp.dot(s.astype(jnp.bfloat16), vc, preferred_element_type=jnp.float32)
        if n > 0:
            o = o + jnp.dot(qc, R.astype(jnp.bfloat16), preferred_element_type=jnp.float32) * xi
        g = g_ref[rs:rs + C, :].astype(jnp.float32)
        o_ref[rs:rs + C, :] = (_rms(o) * _silu(g)).astype(jnp.bfloat16)
        if n + 1 < seq // C:
            kzt = (kc.astype(jnp.float32) * zeta).T.astype(jnp.bfloat16)
            kv = jnp.dot(kzt, vc, preferred_element_type=jnp.float32)
            R = kv if n == 0 else g_chunk * R + kv


def _retention(proj, batch, seq):
    spec = lambda off: pl.BlockSpec((seq, LANES), lambda b, h: (b, off + h))
    return pl.pallas_call(
        functools.partial(_retention_kernel, seq=seq),
        grid=(batch, RET_HEADS),
        in_specs=[spec(_Q_R), spec(_K_R), spec(_V_R), spec(_G_R)],
        out_specs=pl.BlockSpec((seq, LANES), lambda b, h: (b, h)),
        out_shape=jax.ShapeDtypeStruct((batch * seq, RET_HEADS * LANES), jnp.bfloat16),
        compiler_params=pltpu.CompilerParams(
            dimension_semantics=("arbitrary", "arbitrary"), vmem_limit_bytes=VMEM_LIMIT),
        name="retention",
    )(proj, proj, proj, proj)


def _outproj_kernel(ya_ref, yb_ref, w_ref, gpost_ref, x_ref, o_ref):
    half = w_ref.shape[0] // 2
    y = (jnp.dot(ya_ref[...], w_ref[:half, :], preferred_element_type=jnp.float32)
         + jnp.dot(yb_ref[...], w_ref[half:, :], preferred_element_type=jnp.float32))
    o_ref[...] = x_ref[...] + _rms(y) * gpost_ref[...]


def _outproj(ya, yb, w_bf16, gpost, x2):
    T = x2.shape[0]
    d_mix = w_bf16.shape[0]
    return pl.pallas_call(
        _outproj_kernel,
        grid=(T // TM,),
        in_specs=[
            pl.BlockSpec((TM, d_mix // 2), lambda i: (i, 0)),
            pl.BlockSpec((TM, d_mix // 2), lambda i: (i, 0)),
            pl.BlockSpec((d_mix, D_MODEL), lambda i: (0, 0)),
            pl.BlockSpec((1, D_MODEL), lambda i: (0, 0)),
            pl.BlockSpec((TM, D_MODEL), lambda i: (i, 0)),
        ],
        out_specs=pl.BlockSpec((TM, D_MODEL), lambda i: (i, 0)),
        out_shape=jax.ShapeDtypeStruct((T, D_MODEL), jnp.float32),
        compiler_params=pltpu.CompilerParams(
            dimension_semantics=("arbitrary",), vmem_limit_bytes=VMEM_LIMIT),
        name="outproj",
    )(ya, yb, w_bf16, gpost, x2)


def kernel(x, positions, norm_pre_w, w_in, lambda_q1, lambda_k1, lambda_q2, lambda_k2,
           diff_subln_w, w_out, norm_post_w):
    B, S, D = x.shape
    assert D == D_MODEL and w_in.shape == (1, D_MODEL, D_IN_PROJ) and S % TM == 0
    x2 = x.reshape(B * S, D)
    pos2 = positions.reshape(B * S, 1)
    proj, vt = _inproj(x2, pos2, norm_pre_w, w_in[0].astype(jnp.bfloat16), B, S)
    lamv = jnp.concatenate([lambda_q1, lambda_k1, lambda_q2, lambda_k2], axis=0)
    ya = _diff_attn(lamv, diff_subln_w, proj, vt, B, S)
    yb = _retention(proj, B, S)
    out = _outproj(ya, yb, w_out[0].astype(jnp.bfloat16), norm_post_w, x2)
    return out.reshape(B, S, D)
```

```python
import functools
import math

import jax
import jax.numpy as jnp
from jax import lax
from jax.experimental import pallas as pl
from jax.experimental.pallas import tpu as pltpu

LANES = 128

D_MODEL = 1024
DIFF_HEADS = 4
DIFF_HEAD_DIM = 64
RET_HEADS = 4
RET_QK_DIM = 128
SECTION = 512
N_SECTIONS = 8
D_IN_PROJ = SECTION * N_SECTIONS
ROPE_THETA = 10000.0
NORM_EPS = 1e-6
LAMBDA_INIT = 0.8 - 0.6 * math.exp(-0.3 * 0)
NEG_BIG = -1e30
LOG2_E = math.log2(math.e)

TM = 512
TQ = 256
RET_CHUNK = 128

_Q_D, _K_D, _G_D, _Q_R, _K_R, _V_R, _G_R = (i * (SECTION // LANES) for i in range(7))

VMEM_LIMIT = 56 * 1024 * 1024


def _silu(g):
    return g * (1.0 / (1.0 + jnp.exp(-g)))


def _rms(y):
    return y * lax.rsqrt(jnp.mean(y * y, axis=-1, keepdims=True) + NORM_EPS)


def _rotary_tables(pos):
    lane = lax.broadcasted_iota(jnp.int32, (1, LANES), 1)
    lo64 = lane < 64
    expo = jnp.where(lo64, lane.astype(jnp.float32) * (1.0 / 64.0),
                     (lane & 31).astype(jnp.float32) * (1.0 / 32.0))
    inv_freq = jnp.exp(expo * (-math.log(ROPE_THETA)))
    ang = pos.astype(jnp.float32) * inv_freq
    cos_a = jnp.cos(ang)
    sin_a = jnp.sin(ang)
    cos_sw = pltpu.roll(cos_a, 64, 1)
    sin_sw = pltpu.roll(sin_a, 64, 1)
    sin_dd = jnp.where(lo64, sin_sw, sin_a)
    return (jnp.where(lo64, cos_a, cos_sw), jnp.where(lo64, -sin_a, sin_sw),
            jnp.where(lo64, cos_sw, cos_a), jnp.where((lane & 32) == 0, -sin_dd, sin_dd))


def _inproj_kernel(x_ref, pos_ref, gpre_ref, w_ref, proj_ref, vt_ref):
    x = x_ref[...]
    u = (_rms(x) * gpre_ref[...]).astype(jnp.bfloat16)
    lo32 = (lax.broadcasted_iota(jnp.int32, (1, LANES), 1) & 32) == 0
    cos_r, sin_r, cos_d, sin_d = _rotary_tables(pos_ref[...])

    def section(sec):
        return jnp.dot(u, w_ref[:, sec * SECTION:(sec + 1) * SECTION],
                       preferred_element_type=jnp.float32)

    def rot_diff(blk):
        sw = jnp.where(lo32, pltpu.roll(blk, LANES - 32, 1), pltpu.roll(blk, 32, 1))
        return blk * cos_d + sw * sin_d

    def rot_ret(blk):
        return blk * cos_r + pltpu.roll(blk, 64, 1) * sin_r

    def store(t, dst_sec, fn):
        for hb in range(SECTION // LANES):
            blk = fn(t[:, hb * LANES:(hb + 1) * LANES])
            c0 = dst_sec * SECTION + hb * LANES
            proj_ref[:, c0:c0 + LANES] = blk.astype(jnp.bfloat16)

    ident = lambda b: b
    vt_ref[...] = section(2).T.astype(jnp.bfloat16)
    store(section(3), 2, ident)
    store(section(0), 0, lambda b: rot_diff(b) * (DIFF_HEAD_DIM ** -0.5 * LOG2_E))
    store(section(1), 1, rot_diff)
    store(section(4), 3, rot_ret)
    store(section(5), 4, lambda b: rot_ret(b) * (RET_QK_DIM ** -0.5))
    store(section(6), 5, ident)
    store(section(7), 6, ident)


def _inproj(x2, pos2, gpre, w_bf16, batch, seq):
    T = x2.shape[0]
    nblk = seq // TM
    return pl.pallas_call(
        _inproj_kernel,
        grid=(T // TM,),
        in_specs=[
            pl.BlockSpec((TM, D_MODEL), lambda i: (i, 0)),
            pl.BlockSpec((TM, 1), lambda i: (i, 0)),
            pl.BlockSpec((1, D_MODEL), lambda i: (0, 0)),
            pl.BlockSpec((D_MODEL, D_IN_PROJ), lambda i: (0, 0)),
        ],
        out_specs=[
            pl.BlockSpec((TM, 7 * SECTION), lambda i: (i, 0)),
            pl.BlockSpec((None, SECTION, TM), lambda i: (i // nblk, 0, i % nblk)),
        ],
        out_shape=[
            jax.ShapeDtypeStruct((T, 7 * SECTION), jnp.bfloat16),
            jax.ShapeDtypeStruct((batch, SECTION, seq), jnp.bfloat16),
        ],
        compiler_params=pltpu.CompilerParams(
            dimension_semantics=("arbitrary",), vmem_limit_bytes=VMEM_LIMIT),
        name="inproj",
    )(x2, pos2, gpre, w_bf16)


def _diff_attn_kernel(lamv_ref, w_ref, q_ref, k_ref, vt_ref, g_ref, o_ref, s_ref, *, seq):
    lv = lamv_ref[...]
    lam = (jnp.exp(jnp.sum(lv[0:1] * lv[1:2], axis=-1, keepdims=True))
           - jnp.exp(jnp.sum(lv[2:3] * lv[3:4], axis=-1, keepdims=True)) + LAMBDA_INIT)

    lane_q = lax.broadcasted_iota(jnp.int32, (TQ, LANES), 1)
    causal = (lax.broadcasted_iota(jnp.int32, (TQ, TQ), 0)
              <= lax.broadcasted_iota(jnp.int32, (TQ, TQ), 1))
    causal2 = jnp.concatenate([causal, causal], axis=1)
    w_sub = w_ref[...] * (1.0 - LAMBDA_INIT)
    nt = (((1,), (1,)), ((), ()))

    def scores(qi):
        qs = qi * TQ
        q = q_ref[qs:qs + TQ, :]
        zero = jnp.zeros_like(q)
        qq = jnp.concatenate([jnp.where(lane_q < DIFF_HEAD_DIM, q, zero),
                              jnp.where(lane_q >= DIFF_HEAD_DIM, q, zero)], axis=0)
        s_ref[qi % 2, 0:qs + TQ, :] = lax.dot_general(
            k_ref[0:qs + TQ, :], qq, nt, preferred_element_type=jnp.float32)

    def pv(kv0, kv1, p):
        lhs = jnp.concatenate([vt_ref[:, kv0:kv1], jnp.ones((16, kv1 - kv0), jnp.bfloat16)], axis=0)
        return jnp.dot(lhs, p.astype(jnp.bfloat16), preferred_element_type=jnp.float32)

    def finish(qi):
        qs = qi * TQ
        s_d = jnp.where(causal2, s_ref[qi % 2, qs:qs + TQ, :], NEG_BIG)
        s_o = s_ref[qi % 2, 0:qs, :] if qi > 0 else None
        m = jnp.max(s_d, axis=0, keepdims=True)
        if s_o is not None:
            m = jnp.maximum(m, jnp.max(s_o, axis=0, keepdims=True))
        acc = pv(qs, qs + TQ, jnp.exp2(s_d - m))
        if s_o is not None:
            acc = acc + pv(0, qs, jnp.exp2(s_o - m))
        o = acc[:LANES] * (1.0 / acc[LANES:LANES + 1])
        y = (o[:, :TQ] - lam * o[:, TQ:]).T
        y = _rms(y) * w_sub
        g = g_ref[qs:qs + TQ, :].astype(jnp.float32)
        o_ref[qs:qs + TQ, :] = (y * _silu(g)).astype(jnp.bfloat16)

    nq = seq // TQ
    scores(0)
    for qi in range(nq):
        if qi + 1 < nq:
            scores(qi + 1)
        finish(qi)


def _diff_attn(lamv, subln_w, proj, vt, batch, seq):
    return pl.pallas_call(
        functools.partial(_diff_attn_kernel, seq=seq),
        grid=(batch, DIFF_HEADS),
        in_specs=[
            pl.BlockSpec((4, DIFF_HEAD_DIM), lambda b, h: (0, 0)),
            pl.BlockSpec((1, LANES), lambda b, h: (0, 0)),
            pl.BlockSpec((seq, LANES), lambda b, h: (b, _Q_D + h)),
            pl.BlockSpec((seq, LANES), lambda b, h: (b, _K_D + h)),
            pl.BlockSpec((None, LANES, seq), lambda b, h: (b, h, 0)),
            pl.BlockSpec((seq, LANES), lambda b, h: (b, _G_D + h)),
        ],
        out_specs=pl.BlockSpec((seq, LANES), lambda b, h: (b, h)),
        out_shape=jax.ShapeDtypeStruct((batch * seq, DIFF_HEADS * LANES), jnp.bfloat16),
        scratch_shapes=[pltpu.VMEM((2, seq, 2 * TQ), jnp.float32)],
        compiler_params=pltpu.CompilerParams(
            dimension_semantics=("arbitrary", "arbitrary"), vmem_limit_bytes=VMEM_LIMIT),
        name="diff_attn",
    )(lamv, subln_w, proj, proj, vt, proj)


def _retention_kernel(q_ref, k_ref, v_ref, g_ref, o_ref, *, seq):
    C = RET_CHUNK
    h = pl.program_id(1)
    pow2 = jnp.full((1, 1), lax.shift_left(jnp.int32(1), 5 + h), jnp.int32).astype(jnp.float32)
    log_g = jnp.log(1.0 - 1.0 / pow2)
    ri = lax.broadcasted_iota(jnp.int32, (C, C), 0)
    ci = lax.broadcasted_iota(jnp.int32, (C, C), 1)
    d = (ri - ci).astype(jnp.float32)
    decay_in = jnp.where(d >= 0, jnp.exp(jnp.maximum(d, 0.0) * log_g), 0.0)
    rf = lax.broadcasted_iota(jnp.int32, (C, LANES), 0).astype(jnp.float32)
    zeta = jnp.exp((C - 1 - rf) * log_g)
    xi = jnp.exp((rf + 1.0) * log_g)
    g_chunk = jnp.exp(C * log_g)
    nt = (((1,), (1,)), ((), ()))

    n_chunks = seq // C
    rows = lambda ref, n: ref[n * C:(n + 1) * C, :]
    kvs = []
    for n in range(n_chunks - 1):
        kzt = (rows(k_ref, n).astype(jnp.float32) * zeta).T.astype(jnp.bfloat16)
        kvs.append(jnp.dot(kzt, rows(v_ref, n), preferred_element_type=jnp.float32))
    states = [None, kvs[0]]
    for n in range(2, n_chunks):
        states.append(g_chunk * states[-1] + kvs[n - 1])
    inner = [(lax.dot_general(rows(q_ref, n), rows(k_ref, n), nt, preferred_element_type=jnp.float32)
              * decay_in).astype(jnp.bfloat16) for n in range(n_chunks)]
    for n in range(n_chunks):
        o = jnp.dot(inner[n], rows(v_ref, n), preferred_element_type=jnp.float32)
        if n > 0:
            o = o + jnp.dot(rows(q_ref, n), states[n].astype(jnp.bfloat16),
                            preferred_element_type=jnp.float32) * xi
        g = rows(g_ref, n).astype(jnp.float32)
        o_ref[n * C:(n + 1) * C, :] = (_rms(o) * _silu(g)).astype(jnp.bfloat16)


def _retention(proj, batch, seq):
    spec = lambda off: pl.BlockSpec((seq, LANES), lambda b, h: (b, off + h))
    return pl.pallas_call(
        functools.partial(_retention_kernel, seq=seq),
        grid=(batch, RET_HEADS),
        in_specs=[spec(_Q_R), spec(_K_R), spec(_V_R), spec(_G_R)],
        out_specs=pl.BlockSpec((seq, LANES), lambda b, h: (b, h)),
        out_shape=jax.ShapeDtypeStruct((batch * seq, RET_HEADS * LANES), jnp.bfloat16),
        compiler_params=pltpu.CompilerParams(
            dimension_semantics=("arbitrary", "arbitrary"), vmem_limit_bytes=VMEM_LIMIT),
        name="retention",
    )(proj, proj, proj, proj)


def _outproj_kernel(ya_ref, yb_ref, w_ref, gpost_ref, x_ref, o_ref):
    half = w_ref.shape[0] // 2
    y = (jnp.dot(ya_ref[...], w_ref[:half, :], preferred_element_type=jnp.float32)
         + jnp.dot(yb_ref[...], w_ref[half:, :], preferred_element_type=jnp.float32))
    o_ref[...] = x_ref[...] + _rms(y) * gpost_ref[...]


def _outproj(ya, yb, w_bf16, gpost, x2):
    T = x2.shape[0]
    d_mix = w_bf16.shape[0]
    return pl.pallas_call(
        _outproj_kernel,
        grid=(T // TM,),
        in_specs=[
            pl.BlockSpec((TM, d_mix // 2), lambda i: (i, 0)),
            pl.BlockSpec((TM, d_mix // 2), lambda i: (i, 0)),
            pl.BlockSpec((d_mix, D_MODEL), lambda i: (0, 0)),
            pl.BlockSpec((1, D_MODEL), lambda i: (0, 0)),
            pl.BlockSpec((TM, D_MODEL), lambda i: (i, 0)),
        ],
        out_specs=pl.BlockSpec((TM, D_MODEL), lambda i: (i, 0)),
        out_shape=jax.ShapeDtypeStruct((T, D_MODEL), jnp.float32),
        compiler_params=pltpu.CompilerParams(
            dimension_semantics=("arbitrary",), vmem_limit_bytes=VMEM_LIMIT),
        name="outproj",
    )(ya, yb, w_bf16, gpost, x2)


def kernel(x, positions, norm_pre_w, w_in, lambda_q1, lambda_k1, lambda_q2, lambda_k2,
           diff_subln_w, w_out, norm_post_w):
    B, S, D = x.shape
    assert D == D_MODEL and w_in.shape == (1, D_MODEL, D_IN_PROJ) and S % TM == 0
    x2 = x.reshape(B * S, D)
    pos2 = positions.reshape(B * S, 1)
    proj, vt = _inproj(x2, pos2, norm_pre_w, w_in[0].astype(jnp.bfloat16), B, S)
    lamv = jnp.concatenate([lambda_q1, lambda_k1, lambda_q2, lambda_k2], axis=0)
    ya = _diff_attn(lamv, diff_subln_w, proj, vt, B, S)
    yb = _retention(proj, B, S)
    out = _outproj(ya, yb, w_out[0].astype(jnp.bfloat16), norm_post_w, x2)
    return out.reshape(B, S, D)
```

```python
import functools
import math

import jax
import jax.numpy as jnp
from jax import lax
from jax.experimental import pallas as pl
from jax.experimental.pallas import tpu as pltpu

LANES = 128

D_MODEL = 1024
DIFF_HEADS = 4
DIFF_HEAD_DIM = 64
RET_HEADS = 4
RET_QK_DIM = 128
SECTION = 512
N_SECTIONS = 8
D_IN_PROJ = SECTION * N_SECTIONS
ROPE_THETA = 10000.0
NORM_EPS = 1e-6
LAMBDA_INIT = 0.8 - 0.6 * math.exp(-0.3 * 0)
NEG_BIG = -1e30
LOG2_E = math.log2(math.e)

TM = 1024
TQ = 256
RET_CHUNK = 128

_Q_D, _K_D, _G_D, _Q_R, _K_R, _V_R, _G_R = (i * (SECTION // LANES) for i in range(7))

VMEM_LIMIT = 56 * 1024 * 1024


def _silu(g):
    return g * (1.0 / (1.0 + jnp.exp(-g)))


def _rms(y):
    return y * lax.rsqrt(jnp.mean(y * y, axis=-1, keepdims=True) + NORM_EPS)


def _inproj_kernel(x_ref, pos_ref, pos_dense_ref, gpre_ref, w_ref, proj_ref, vt_ref,
                   base_ref, cs_ref):
    lane = lax.broadcasted_iota(jnp.int32, (1, LANES), 1)
    lo64 = lane < 64
    lo32 = (lane & 32) == 0
    expo = jnp.where(lo64, lane.astype(jnp.float32) * (1.0 / 64.0),
                     (lane & 31).astype(jnp.float32) * (1.0 / 32.0))
    inv_freq = jnp.exp(expo * (-math.log(ROPE_THETA)))

    @pl.when(pl.program_id(0) == 0)
    def _():
        ang = lax.broadcasted_iota(jnp.int32, (TM, 1), 0).astype(jnp.float32) * inv_freq
        base_ref[0] = jnp.cos(ang)
        base_ref[1] = jnp.sin(ang)

    pos_dense = pos_dense_ref[...]
    p0 = pos_dense[0:1, 0:1]
    offs = (lax.broadcasted_iota(jnp.int32, pos_dense.shape, 0) * LANES
            + lax.broadcasted_iota(jnp.int32, pos_dense.shape, 1))
    consecutive = jnp.min((pos_dense == p0 + offs).astype(jnp.int32)) == 1

    @pl.when(consecutive)
    def _():
        a0 = p0.astype(jnp.float32) * inv_freq
        c0, s0 = jnp.cos(a0), jnp.sin(a0)
        cr, sr = base_ref[0], base_ref[1]
        cs_ref[0] = c0 * cr - s0 * sr
        cs_ref[1] = s0 * cr + c0 * sr

    @pl.when(jnp.logical_not(consecutive))
    def _():
        ang = pos_ref[...].astype(jnp.float32) * inv_freq
        cs_ref[0] = jnp.cos(ang)
        cs_ref[1] = jnp.sin(ang)

    x = x_ref[...]
    u = (_rms(x) * gpre_ref[...]).astype(jnp.bfloat16)
    cos_a, sin_a = cs_ref[0], cs_ref[1]
    cos_sw = pltpu.roll(cos_a, 64, 1)
    sin_sw = pltpu.roll(sin_a, 64, 1)
    cos_r = jnp.where(lo64, cos_a, cos_sw)
    sin_r = jnp.where(lo64, -sin_a, sin_sw)
    cos_d = jnp.where(lo64, cos_sw, cos_a)
    sin_dd = jnp.where(lo64, sin_sw, sin_a)
    sin_d = jnp.where(lo32, -sin_dd, sin_dd)

    def section(sec):
        return jnp.dot(u, w_ref[:, sec * SECTION:(sec + 1) * SECTION],
                       preferred_element_type=jnp.float32)

    def rot_diff(blk):
        sw = jnp.where(lo32, pltpu.roll(blk, LANES - 32, 1), pltpu.roll(blk, 32, 1))
        return blk * cos_d + sw * sin_d

    def rot_ret(blk):
        return blk * cos_r + pltpu.roll(blk, 64, 1) * sin_r

    def store(t, dst_sec, fn):
        for hb in range(SECTION // LANES):
            blk = fn(t[:, hb * LANES:(hb + 1) * LANES])
            c0 = dst_sec * SECTION + hb * LANES
            proj_ref[:, c0:c0 + LANES] = blk.astype(jnp.bfloat16)

    ident = lambda b: b
    vt_ref[...] = section(2).T.astype(jnp.bfloat16)
    store(section(3), 2, ident)
    store(section(0), 0, lambda b: rot_diff(b) * (DIFF_HEAD_DIM ** -0.5 * LOG2_E))
    store(section(1), 1, rot_diff)
    store(section(4), 3, rot_ret)
    store(section(5), 4, lambda b: rot_ret(b) * (RET_QK_DIM ** -0.5))
    store(section(6), 5, ident)
    store(section(7), 6, ident)


def _inproj(x2, positions, gpre, w_bf16, batch, seq):
    T = x2.shape[0]
    nblk = seq // TM
    pos2 = positions.reshape(T, 1)
    pos_dense = positions.reshape(T // TM, TM // LANES, LANES)
    return pl.pallas_call(
        _inproj_kernel,
        grid=(T // TM,),
        in_specs=[
            pl.BlockSpec((TM, D_MODEL), lambda i: (i, 0)),
            pl.BlockSpec((TM, 1), lambda i: (i, 0)),
            pl.BlockSpec((None, TM // LANES, LANES), lambda i: (i, 0, 0)),
            pl.BlockSpec((1, D_MODEL), lambda i: (0, 0)),
            pl.BlockSpec((D_MODEL, D_IN_PROJ), lambda i: (0, 0)),
        ],
        out_specs=[
            pl.BlockSpec((TM, 7 * SECTION), lambda i: (i, 0)),
            pl.BlockSpec((None, SECTION, TM), lambda i: (i // nblk, 0, i % nblk)),
        ],
        out_shape=[
            jax.ShapeDtypeStruct((T, 7 * SECTION), jnp.bfloat16),
            jax.ShapeDtypeStruct((batch, SECTION, seq), jnp.bfloat16),
        ],
        scratch_shapes=[pltpu.VMEM((2, TM, LANES), jnp.float32),
                        pltpu.VMEM((2, TM, LANES), jnp.float32)],
        compiler_params=pltpu.CompilerParams(
            dimension_semantics=("arbitrary",), vmem_limit_bytes=VMEM_LIMIT),
        name="inproj",
    )(x2, pos2, pos_dense, gpre, w_bf16)


def _diff_attn_kernel(lamv_ref, w_ref, q_ref, k_ref, vt_ref, g_ref, o_ref, s_ref, *, seq):
    lv = lamv_ref[...]
    lam = (jnp.exp(jnp.sum(lv[0:1] * lv[1:2], axis=-1, keepdims=True))
           - jnp.exp(jnp.sum(lv[2:3] * lv[3:4], axis=-1, keepdims=True)) + LAMBDA_INIT)

    lane_q = lax.broadcasted_iota(jnp.int32, (TQ, LANES), 1)
    causal = (lax.broadcasted_iota(jnp.int32, (TQ, TQ), 0)
              <= lax.broadcasted_iota(jnp.int32, (TQ, TQ), 1))
    causal2 = jnp.concatenate([causal, causal], axis=1)
    w_sub = w_ref[...] * (1.0 - LAMBDA_INIT)
    nt = (((1,), (1,)), ((), ()))

    def scores(qi):
        qs = qi * TQ
        q = q_ref[qs:qs + TQ, :]
        zero = jnp.zeros_like(q)
        qq = jnp.concatenate([jnp.where(lane_q < DIFF_HEAD_DIM, q, zero),
                              jnp.where(lane_q >= DIFF_HEAD_DIM, q, zero)], axis=0)
        s_ref[qi % 2, 0:qs + TQ, :] = lax.dot_general(
            k_ref[0:qs + TQ, :], qq, nt, preferred_element_type=jnp.float32)

    def pv(kv0, kv1, p):
        lhs = jnp.concatenate([vt_ref[:, kv0:kv1], jnp.ones((16, kv1 - kv0), jnp.bfloat16)], axis=0)
        return jnp.dot(lhs, p.astype(jnp.bfloat16), preferred_element_type=jnp.float32)

    def finish(qi):
        qs = qi * TQ
        s_d = jnp.where(causal2, s_ref[qi % 2, qs:qs + TQ, :], NEG_BIG)
        s_o = s_ref[qi % 2, 0:qs, :] if qi > 0 else None
        m = jnp.max(s_d, axis=0, keepdims=True)
        if s_o is not None:
            m = jnp.maximum(m, jnp.max(s_o, axis=0, keepdims=True))
        acc = pv(qs, qs + TQ, jnp.exp2(s_d - m))
        if s_o is not None:
            acc = acc + pv(0, qs, jnp.exp2(s_o - m))
        o = acc[:LANES] * (1.0 / acc[LANES:LANES + 1])
        y = (o[:, :TQ] - lam * o[:, TQ:]).T
        y = _rms(y) * w_sub
        g = g_ref[qs:qs + TQ, :].astype(jnp.float32)
        o_ref[qs:qs + TQ, :] = (y * _silu(g)).astype(jnp.bfloat16)

    nq = seq // TQ
    scores(0)
    for qi in range(nq):
        if qi + 1 < nq:
            scores(qi + 1)
        finish(qi)


def _diff_attn(lamv, subln_w, proj, vt, batch, seq):
    return pl.pallas_call(
        functools.partial(_diff_attn_kernel, seq=seq),
        grid=(batch, DIFF_HEADS),
        in_specs=[
            pl.BlockSpec((4, DIFF_HEAD_DIM), lambda b, h: (0, 0)),
            pl.BlockSpec((1, LANES), lambda b, h: (0, 0)),
            pl.BlockSpec((seq, LANES), lambda b, h: (b, _Q_D + h)),
            pl.BlockSpec((seq, LANES), lambda b, h: (b, _K_D + h)),
            pl.BlockSpec((None, LANES, seq), lambda b, h: (b, h, 0)),
            pl.BlockSpec((seq, LANES), lambda b, h: (b, _G_D + h)),
        ],
        out_specs=pl.BlockSpec((seq, LANES), lambda b, h: (b, h)),
        out_shape=jax.ShapeDtypeStruct((batch * seq, DIFF_HEADS * LANES), jnp.bfloat16),
        scratch_shapes=[pltpu.VMEM((2, seq, 2 * TQ), jnp.float32)],
        compiler_params=pltpu.CompilerParams(
            dimension_semantics=("arbitrary", "arbitrary"), vmem_limit_bytes=VMEM_LIMIT),
        name="diff_attn",
    )(lamv, subln_w, proj, proj, vt, proj)


def _retention_kernel(q_ref, k_ref, v_ref, g_ref, o_ref, *, seq):
    C = RET_CHUNK
    h = pl.program_id(1)
    pow2 = jnp.full((1, 1), lax.shift_left(jnp.int32(1), 5 + h), jnp.int32).astype(jnp.float32)
    log_g = jnp.log(1.0 - 1.0 / pow2)
    ri = lax.broadcasted_iota(jnp.int32, (C, C), 0)
    ci = lax.broadcasted_iota(jnp.int32, (C, C), 1)
    d = (ri - ci).astype(jnp.float32)
    decay_in = jnp.where(d >= 0, jnp.exp(jnp.maximum(d, 0.0) * log_g), 0.0)
    rf = lax.broadcasted_iota(jnp.int32, (C, LANES), 0).astype(jnp.float32)
    zeta = jnp.exp((C - 1 - rf) * log_g)
    xi = jnp.exp((rf + 1.0) * log_g)
    g_chunk = jnp.exp(C * log_g)
    nt = (((1,), (1,)), ((), ()))

    n_chunks = seq // C
    rows = lambda ref, n: ref[n * C:(n + 1) * C, :]
    kvs = []
    for n in range(n_chunks - 1):
        kzt = (rows(k_ref, n).astype(jnp.float32) * zeta).T.astype(jnp.bfloat16)
        kvs.append(jnp.dot(kzt, rows(v_ref, n), preferred_element_type=jnp.float32))
    states = [None, kvs[0]]
    for n in range(2, n_chunks):
        states.append(g_chunk * states[-1] + kvs[n - 1])
    inner = [(lax.dot_general(rows(q_ref, n), rows(k_ref, n), nt, preferred_element_type=jnp.float32)
              * decay_in).astype(jnp.bfloat16) for n in range(n_chunks)]
    for n in range(n_chunks):
        o = jnp.dot(inner[n], rows(v_ref, n), preferred_element_type=jnp.float32)
        if n > 0:
            o = o + jnp.dot(rows(q_ref, n), states[n].astype(jnp.bfloat16),
                            preferred_element_type=jnp.float32) * xi
        g = rows(g_ref, n).astype(jnp.float32)
        o_ref[n * C:(n + 1) * C, :] = (_rms(o) * _silu(g)).astype(jnp.bfloat16)


def _retention(proj, batch, seq):
    spec = lambda off: pl.BlockSpec((seq, LANES), lambda b, h: (b, off + h))
    return pl.pallas_call(
        functools.partial(_retention_kernel, seq=seq),
        grid=(batch, RET_HEADS),
        in_specs=[spec(_Q_R), spec(_K_R), spec(_V_R), spec(_G_R)],
        out_specs=pl.BlockSpec((seq, LANES), lambda b, h: (b, h)),
        out_shape=jax.ShapeDtypeStruct((batch * seq, RET_HEADS * LANES), jnp.bfloat16),
        compiler_params=pltpu.CompilerParams(
            dimension_semantics=("arbitrary", "arbitrary"), vmem_limit_bytes=VMEM_LIMIT),
        name="retention",
    )(proj, proj, proj, proj)


def _outproj_kernel(ya_ref, yb_ref, w_ref, gpost_ref, x_ref, o_ref):
    half = w_ref.shape[0] // 2
    y = (jnp.dot(ya_ref[...], w_ref[:half, :], preferred_element_type=jnp.float32)
         + jnp.dot(yb_ref[...], w_ref[half:, :], preferred_element_type=jnp.float32))
    o_ref[...] = x_ref[...] + _rms(y) * gpost_ref[...]


def _outproj(ya, yb, w_bf16, gpost, x2):
    T = x2.shape[0]
    d_mix = w_bf16.shape[0]
    return pl.pallas_call(
        _outproj_kernel,
        grid=(T // TM,),
        in_specs=[
            pl.BlockSpec((TM, d_mix // 2), lambda i: (i, 0)),
            pl.BlockSpec((TM, d_mix // 2), lambda i: (i, 0)),
            pl.BlockSpec((d_mix, D_MODEL), lambda i: (0, 0)),
            pl.BlockSpec((1, D_MODEL), lambda i: (0, 0)),
            pl.BlockSpec((TM, D_MODEL), lambda i: (i, 0)),
        ],
        out_specs=pl.BlockSpec((TM, D_MODEL), lambda i: (i, 0)),
        out_shape=jax.ShapeDtypeStruct((T, D_MODEL), jnp.float32),
        compiler_params=pltpu.CompilerParams(
            dimension_semantics=("arbitrary",), vmem_limit_bytes=VMEM_LIMIT),
        name="outproj",
    )(ya, yb, w_bf16, gpost, x2)


def kernel(x, positions, norm_pre_w, w_in, lambda_q1, lambda_k1, lambda_q2, lambda_k2,
           diff_subln_w, w_out, norm_post_w):
    B, S, D = x.shape
    assert D == D_MODEL and w_in.shape == (1, D_MODEL, D_IN_PROJ) and S % TM == 0
    x2 = x.reshape(B * S, D)
    proj, vt = _inproj(x2, positions, norm_pre_w, w_in[0].astype(jnp.bfloat16), B, S)
    lamv = jnp.concatenate([lambda_q1, lambda_k1, lambda_q2, lambda_k2], axis=0)
    ya = _diff_attn(lamv, diff_subln_w, proj, vt, B, S)
    yb = _retention(proj, B, S)
    out = _outproj(ya, yb, w_out[0].astype(jnp.bfloat16), norm_post_w, x2)
    return out.reshape(B, S, D)
```
